```python
import math
import jax
import jax.numpy as jnp
from jax import lax
import numpy as np


D_MODEL = 1024
BATCH = 8
SEQ = 4096
DEPTH = 2

HEAD_DIM = 64
D_MIX = D_MODEL
D_SSM = D_MIX // 4
D_ATTN = D_MIX // 2
D_CONV = D_MIX - D_SSM - D_ATTN
SSM_GROUP = 16
N_SSM_GROUPS = D_SSM // SSM_GROUP
SSM_STATE = 64
DT_MIN = 1e-3
DT_MAX = 1e-1
N_ATTN_HEADS = D_ATTN // HEAD_DIM
Q_BLOCK = 128
CONV_WIDTH = 3
N_CONV_HEADS = D_CONV // HEAD_DIM
N_OUT_HEADS = D_MIX // HEAD_DIM
D_IN = D_SSM + 3 * D_ATTN + N_ATTN_HEADS + 3 * D_CONV
N_EXPERTS = 16
N_GROUPS = 4
EXPERTS_PER_GROUP = N_EXPERTS // N_GROUPS
TOP_K = 2
D_EXPERT = D_MODEL // 2
EPS = 1e-6

kernel_name = "hymba_s5_fox_shortconv_grouped_moe"


def rms_norm(x, g):
    xf = x.astype(jnp.float32)
    y = xf * lax.rsqrt(jnp.mean(xf * xf, axis=-1, keepdims=True) + EPS)
    return (y * g.astype(jnp.float32)).astype(x.dtype)


def modulate(h, shift, scale):
    return h * (1 + scale[:, None, :]) + shift[:, None, :]


def s5_mixer(u, lam_re, lam_im, log_dt, b_re, b_im, c_re, c_im, d_skip, glu_w, glu_b):
    f32 = jnp.float32
    bsz, seq, _ = u.shape
    uf = u.astype(f32).reshape(bsz, seq, N_SSM_GROUPS, SSM_GROUP)
    lr = jnp.minimum(lam_re.astype(f32), -1e-4)
    li = lam_im.astype(f32)
    dt = jnp.exp(log_dt.astype(f32))[:, None]
    mag = jnp.exp(lr * dt)
    ab_re = mag * jnp.cos(li * dt)
    ab_im = mag * jnp.sin(li * dt)
    den = lr * lr + li * li
    num_re = ab_re - 1.0
    num_im = ab_im
    z_re = (num_re * lr + num_im * li) / den
    z_im = (num_im * lr - num_re * li) / den
    br = b_re.astype(f32)
    bi = b_im.astype(f32)
    bb_re = z_re[..., None] * br - z_im[..., None] * bi
    bb_im = z_re[..., None] * bi + z_im[..., None] * br
    bu_re = jnp.einsum('gph,bsgh->bsgp', bb_re, uf)
    bu_im = jnp.einsum('gph,bsgh->bsgp', bb_im, uf)
    a_re = jnp.broadcast_to(ab_re, bu_re.shape)
    a_im = jnp.broadcast_to(ab_im, bu_im.shape)

    def combine(e1, e2):
        a1r, a1i, b1r, b1i = e1
        a2r, a2i, b2r, b2i = e2
        return (a2r * a1r - a2i * a1i,
                a2r * a1i + a2i * a1r,
                a2r * b1r - a2i * b1i + b2r,
                a2r * b1i + a2i * b1r + b2i)

    _, _, x_re, x_im = lax.associative_scan(combine, (a_re, a_im, bu_re, bu_im), axis=1)
    y = (jnp.einsum('ghp,bsgp->bsgh', c_re.astype(f32), x_re)
         - jnp.einsum('ghp,bsgp->bsgh', c_im.astype(f32), x_im))
    y = y.reshape(bsz, seq, D_SSM) + d_skip.astype(f32) * uf.reshape(bsz, seq, D_SSM)
    y = jax.nn.gelu(y)
    y = y * jax.nn.sigmoid(y @ glu_w.astype(f32) + glu_b.astype(f32))
    return y.astype(u.dtype)


def fox_attention(q, k, v, f_logit, q_g, k_g):
    f32 = jnp.float32
    bsz, seq, n_heads, hd = q.shape
    q = rms_norm(q, q_g)
    k = rms_norm(k, k_g)
    log_f = jax.nn.log_sigmoid(f_logit.astype(f32))
    cum = jnp.cumsum(log_f, axis=1).transpose(0, 2, 1)
    n_blocks = seq // Q_BLOCK
    qb = q.reshape(bsz, n_blocks, Q_BLOCK, n_heads, hd).transpose(1, 0, 3, 2, 4)
    cqb = cum.reshape(bsz, n_heads, n_blocks, Q_BLOCK).transpose(2, 0, 1, 3)
    key_pos = jnp.arange(seq)
    scale = HEAD_DIM ** -0.5

    def block(args):
        q_blk, cq_blk, blk = args
        logits = jnp.einsum('bhqd,bkhd->bhqk', q_blk, k, preferred_element_type=f32) * scale
        logits = logits + cq_blk[..., None] - cum[:, :, None, :]
        q_pos = blk * Q_BLOCK + jnp.arange(Q_BLOCK)
        mask = key_pos[None, :] <= q_pos[:, None]
        logits = jnp.where(mask[None, None], logits, -jnp.inf)
        p = jax.nn.softmax(logits, axis=-1)
        return jnp.einsum('bhqk,bkhd->bqhd', p.astype(v.dtype), v)

    out = lax.map(block, (qb, cqb, jnp.arange(n_blocks)))
    return out.transpose(1, 0, 2, 3, 4).reshape(bsz, seq, n_heads * hd)


def short_conv(h, b_gate, c_gate, conv_w):
    z = c_gate * h
    zp = jnp.pad(z, ((0, 0), (CONV_WIDTH - 1, 0), (0, 0)))
    y = lax.conv_general_dilated(zp, conv_w[:, None, :].astype(z.dtype), window_strides=(1,),
                                 padding='VALID', dimension_numbers=('NWC', 'WIO', 'NWC'),
                                 feature_group_count=D_CONV)
    return b_gate * y


def mixer_layer(h, w_in, forget_b, lam_re, lam_im, log_dt, b_re, b_im, c_re, c_im, d_skip,
                glu_w, glu_b, q_g, k_g, conv_w, out_norm_g, w_out):
    bsz, seq, _ = h.shape
    proj = h @ w_in
    sizes = (D_SSM, D_ATTN, D_ATTN, D_ATTN, N_ATTN_HEADS, D_CONV, D_CONV, D_CONV)
    offs = [int(o) for o in np.cumsum(sizes)[:-1]]
    u, q, k, v, f_logit, hc, bg, cg = jnp.split(proj, offs, axis=-1)
    y_ssm = s5_mixer(u, lam_re, lam_im, log_dt, b_re, b_im, c_re, c_im, d_skip, glu_w, glu_b)
    heads = lambda t: t.reshape(bsz, seq, N_ATTN_HEADS, HEAD_DIM)
    y_attn = fox_attention(heads(q), heads(k), heads(v), f_logit + forget_b, q_g, k_g)
    y_conv = short_conv(hc, bg, cg, conv_w)
    y = jnp.concatenate([y_ssm, y_attn.astype(h.dtype), y_conv], axis=-1)
    y = rms_norm(y.reshape(bsz, seq, N_OUT_HEADS, HEAD_DIM),
                 out_norm_g.reshape(N_OUT_HEADS, HEAD_DIM)).reshape(bsz, seq, D_MIX)
    return y @ w_out


def moe_layer(h, w_router, router_bias, w_gate, w_up, w_down):
    f32 = jnp.float32
    bsz, seq, d = h.shape
    t = h.reshape(-1, d)
    affinity = jax.nn.sigmoid(jnp.dot(t, w_router, preferred_element_type=f32))
    sel = affinity + router_bias.astype(f32)
    group_score = lax.top_k(sel.reshape(-1, N_GROUPS, EXPERTS_PER_GROUP), TOP_K)[0].sum(-1)
    best_group = jnp.argmax(group_score, axis=-1)
    in_group = (jnp.arange(N_EXPERTS) // EXPERTS_PER_GROUP)[None, :] == best_group[:, None]
    _, idx = lax.top_k(jnp.where(in_group, sel, -jnp.inf), TOP_K)
    wts = jnp.take_along_axis(affinity, idx, axis=-1)
    wts = wts / jnp.sum(wts, axis=-1, keepdims=True)
    combine = jnp.sum(jax.nn.one_hot(idx, N_EXPERTS, dtype=f32) * wts[..., None], axis=1)
    out = jnp.zeros(t.shape, f32)
    for e in range(N_EXPERTS):
        hid = jax.nn.silu(t @ w_gate[e]) * (t @ w_up[e])
        out = out + combine[:, e:e + 1] * (hid @ w_down[e])
    return out.reshape(bsz, seq, d).astype(h.dtype)


def setup_inputs(seed: int = 0) -> dict:
    key = jax.random.key(seed)
    ks = jax.random.split(key, 32)
    f32 = jnp.float32

    def nrm(k, shape, s):
        return s * jax.random.normal(k, shape, f32)

    gshape = (DEPTH, N_SSM_GROUPS, SSM_STATE)
    n_idx = jnp.arange(SSM_STATE, dtype=f32)
    return {
        'x': nrm(ks[0], (BATCH, SEQ, D_MODEL), 1.0),
        'c': nrm(ks[1], (BATCH, D_MODEL), 1.0),
        'ada_w': nrm(ks[2], (DEPTH, D_MODEL, 6 * D_MODEL), 0.5 * D_MODEL ** -0.5),
        'ada_b': nrm(ks[3], (DEPTH, 6 * D_MODEL), 0.02),
        'norm1_g': 1.0 + nrm(ks[4], (DEPTH, D_MODEL), 0.02),
        'w_in': nrm(ks[5], (DEPTH, D_MODEL, D_IN), D_MODEL ** -0.5),
        'forget_b': 2.0 + nrm(ks[6], (DEPTH, N_ATTN_HEADS), 0.1),
        'lam_re': -0.5 + nrm(ks[7], gshape, 0.01),
        'lam_im': math.pi * n_idx + nrm(ks[8], gshape, 0.01),
        'log_dt': jax.random.uniform(ks[9], (DEPTH, N_SSM_GROUPS), f32, math.log(DT_MIN), math.log(DT_MAX)),
        'ssm_b_re': nrm(ks[10], (DEPTH, N_SSM_GROUPS, SSM_STATE, SSM_GROUP), (2 * SSM_GROUP) ** -0.5),
        'ssm_b_im': nrm(ks[11], (DEPTH, N_SSM_GROUPS, SSM_STATE, SSM_GROUP), (2 * SSM_GROUP) ** -0.5),
        'ssm_c_re': nrm(ks[12], (DEPTH, N_SSM_GROUPS, SSM_GROUP, SSM_STATE), (2 * SSM_STATE) ** -0.5),
        'ssm_c_im': nrm(ks[13], (DEPTH, N_SSM_GROUPS, SSM_GROUP, SSM_STATE), (2 * SSM_STATE) ** -0.5),
        'ssm_d': nrm(ks[14], (DEPTH, D_SSM), 1.0),
        'glu_w': nrm(ks[15], (DEPTH, D_SSM, D_SSM), D_SSM ** -0.5),
        'glu_b': nrm(ks[16], (DEPTH, D_SSM), 0.02),
        'q_norm_g': 1.0 + nrm(ks[17], (DEPTH, HEAD_DIM), 0.02),
        'k_norm_g': 1.0 + nrm(ks[18], (DEPTH, HEAD_DIM), 0.02),
        'conv_w': nrm(ks[19], (DEPTH, CONV_WIDTH, D_CONV), CONV_WIDTH ** -0.5),
        'out_norm_g': 1.0 + nrm(ks[20], (DEPTH, D_MIX), 0.02),
        'w_out': nrm(ks[21], (DEPTH, D_MIX, D_MODEL), D_MIX ** -0.5),
        'norm2_g': 1.0 + nrm(ks[22], (DEPTH, D_MODEL), 0.02),
        'w_router': nrm(ks[23], (D_MODEL, N_EXPERTS), D_MODEL ** -0.5),
        'router_bias': nrm(ks[24], (N_EXPERTS,), 0.01),
        'w_gate': nrm(ks[25], (DEPTH, N_EXPERTS, D_MODEL, D_EXPERT), D_MODEL ** -0.5),
        'w_up': nrm(ks[26], (DEPTH, N_EXPERTS, D_MODEL, D_EXPERT), D_MODEL ** -0.5),
        'w_down': nrm(ks[27], (DEPTH, N_EXPERTS, D_EXPERT, D_MODEL), D_EXPERT ** -0.5),
    }


def reference(x, c, ada_w, ada_b, norm1_g, w_in, forget_b, lam_re, lam_im, log_dt,
              ssm_b_re, ssm_b_im, ssm_c_re, ssm_c_im, ssm_d, glu_w, glu_b, q_norm_g, k_norm_g,
              conv_w, out_norm_g, w_out, norm2_g, w_router, router_bias, w_gate, w_up, w_down):
    c_act = jax.nn.silu(c)
    for l in range(DEPTH):
        mod = c_act @ ada_w[l] + ada_b[l]
        sh1, sc1, g1, sh2, sc2, g2 = jnp.split(mod, 6, axis=-1)
        h = modulate(rms_norm(x, norm1_g[l]), sh1, sc1)
        y = mixer_layer(h, w_in[l], forget_b[l], lam_re[l], lam_im[l], log_dt[l],
                        ssm_b_re[l], ssm_b_im[l], ssm_c_re[l], ssm_c_im[l], ssm_d[l],
                        glu_w[l], glu_b[l], q_norm_g[l], k_norm_g[l], conv_w[l],
                        out_norm_g[l], w_out[l])
        x = x + g1[:, None, :] * y
        h = modulate(rms_norm(x, norm2_g[l]), sh2, sc2)
        x = x + g2[:, None, :] * moe_layer(h, w_router, router_bias, w_gate[l], w_up[l], w_down[l])
    return x
```

```python
import functools
import math

import jax
import jax.numpy as jnp
from jax import lax
from jax.experimental import pallas as pl
from jax.experimental.pallas import tpu as pltpu

F32 = jnp.float32
BF16 = jnp.bfloat16

EPS = 1e-6
HEAD_DIM = 64
SSM_GROUP = 16
EXPERTS_PER_GROUP = 4
N_GROUPS = 4
CONV_WIDTH = 3

V7X_SUBLANES = 8
V7X_LANES = 128
V7X_VMEM_BYTES = 64 * 1024 * 1024


def _cparams(semantics, vmem_mib):
    assert vmem_mib * 1024 * 1024 < V7X_VMEM_BYTES
    return pltpu.CompilerParams(dimension_semantics=semantics,
                                vmem_limit_bytes=vmem_mib * 1024 * 1024)


def _dot(a, b):
    return jnp.dot(a, b, preferred_element_type=F32)


def _split2(a):
    hi = a.astype(BF16)
    lo = (a - hi.astype(F32)).astype(BF16)
    return hi, lo


def _dot_3x(a, b):
    ah, al = _split2(a)
    bh, bl = _split2(b)
    return _dot(ah, bh) + _dot(al, bh) + _dot(ah, bl)


def _dot_exact_lhs(a_bf16, b):
    b0 = b.astype(BF16)
    r1 = b - b0.astype(F32)
    b1 = r1.astype(BF16)
    b2 = (r1 - b1.astype(F32)).astype(BF16)
    return _dot(a_bf16, b0) + _dot(a_bf16, b1) + _dot(a_bf16, b2)


def _dot_exact_rhs(a, b_bf16):
    a0 = a.astype(BF16)
    r1 = a - a0.astype(F32)
    a1 = r1.astype(BF16)
    a2 = (r1 - a1.astype(F32)).astype(BF16)
    return _dot(a0, b_bf16) + _dot(a1, b_bf16) + _dot(a2, b_bf16)


def _per_batch(rows, vec):
    tm, d = rows.shape
    b = vec.shape[0]
    return rows.reshape(tm // b, b, d), vec[None]


def _modulated_norm(x, gain, shift, scale):
    ms = jnp.mean(x * x, axis=-1, keepdims=True)
    y = x * lax.rsqrt(ms + EPS) * gain
    y3, sc = _per_batch(y, 1.0 + scale)
    return (y3 * sc + shift[None]).reshape(x.shape)


def _head_norm(y, e_mat, gain):
    ms = _dot((y * y).astype(BF16), e_mat)
    return y * lax.rsqrt(ms + EPS) * gain


def _group_mean_matrix(n):
    idx = jnp.arange(n) // HEAD_DIM
    return jnp.where(idx[:, None] == idx[None, :], 1.0 / HEAD_DIM, 0.0).astype(BF16)


def _adaln_kernel(c_ref, w_ref, b_ref, o_ref):
    c = c_ref[...]
    o_ref[0] = _dot_3x(jax.nn.silu(c), w_ref[0]) + b_ref[0]


def _adaln(c, ada_w, ada_b):
    depth, d, n = ada_w.shape
    b = c.shape[0]
    tn = 1536 if n % 1536 == 0 else n
    return pl.pallas_call(
        _adaln_kernel,
        grid=(depth, n // tn),
        in_specs=[pl.BlockSpec((b, d), lambda l, j: (0, 0)),
                  pl.BlockSpec((1, d, tn), lambda l, j: (l, 0, j)),
                  pl.BlockSpec((1, 1, tn), lambda l, j: (l, 0, j))],
        out_specs=pl.BlockSpec((1, b, tn), lambda l, j: (l, 0, j)),
        out_shape=jax.ShapeDtypeStruct((depth, b, n), F32),
        compiler_params=_cparams(("arbitrary", "arbitrary"), 48),
        name="adaln",
    )(c, ada_w, ada_b.reshape(depth, 1, n))


def _inproj_kernel(x_ref, g_ref, sh_ref, sc_ref, w_ref,
                   u_ref, q_ref, k_ref, v_ref, hc_ref, bg_ref, cg_ref, f_ref, *, offs):
    h = _modulated_norm(x_ref[...], g_ref[...], sh_ref[...], sc_ref[...]).astype(BF16)
    outs = (u_ref, q_ref, k_ref, v_ref, hc_ref, bg_ref, cg_ref)
    for o_ref, lo, hi in zip(outs, offs[:-1], offs[1:]):
        o_ref[...] = _dot(h, w_ref[:, lo:hi]).astype(o_ref.dtype)
    fl = _dot(h, w_ref[:, offs[-1]:offs[-1] + V7X_LANES])
    f_ref[...] = fl[:, :f_ref.shape[1]]


def _inproj(xs, gain, shift, scale, w_perm, widths, n_heads, tm):
    t, d = xs.shape
    b = shift.shape[0]
    offs = [0]
    for w in widths:
        offs.append(offs[-1] + w)
    dts = (F32, BF16, BF16, BF16, F32, F32, F32)
    row = lambda i: (i, 0)
    fix = lambda i: (0, 0)
    out_shape = [jax.ShapeDtypeStruct((t, w), dt) for w, dt in zip(widths, dts)]
    out_shape.append(jax.ShapeDtypeStruct((t, n_heads), F32))
    out_specs = [pl.BlockSpec((tm, w), row) for w in widths] + [pl.BlockSpec((tm, n_heads), row)]
    return pl.pallas_call(
        functools.partial(_inproj_kernel, offs=tuple(offs)),
        grid=(t // tm,),
        in_specs=[pl.BlockSpec((tm, d), row), pl.BlockSpec((1, d), fix),
                  pl.BlockSpec((b, d), fix), pl.BlockSpec((b, d), fix),
                  pl.BlockSpec(w_perm.shape, fix)],
        out_specs=out_specs,
        out_shape=out_shape,
        compiler_params=_cparams(("arbitrary",), 56),
        name="inproj",
    )(xs, gain.reshape(1, d), shift, scale, w_perm)


def _cum_kernel(f_ref, fb_ref, o_ref, *, blk):
    s, n = f_ref.shape
    r = lax.broadcasted_iota(jnp.int32, (blk, blk), 0)
    c = lax.broadcasted_iota(jnp.int32, (blk, blk), 1)
    tri = jnp.where(r >= c, 1.0, 0.0).astype(BF16)
    carry = jnp.zeros((1, n), F32)
    for i in range(s // blk):
        z = f_ref[i * blk:(i + 1) * blk, :] + fb_ref[...]
        ls = jnp.minimum(z, 0.0) - jnp.log1p(jnp.exp(-jnp.abs(z)))
        cs = _dot_exact_lhs(tri, ls) + carry
        o_ref[i * blk:(i + 1) * blk, :] = cs
        carry = cs[blk - 1:blk, :]


def _forget_cumsum(f2d, fb_row):
    s, n = f2d.shape
    blk = min(512, s)
    return pl.pallas_call(
        functools.partial(_cum_kernel, blk=blk),
        out_shape=jax.ShapeDtypeStruct((s, n), F32),
        name="forget_cumsum",
    )(f2d, fb_row)


def _ssm_conv_kernel(u_ref, hc_ref, bg_ref, cg_ref, wb_ref, wc_ref, a_ref, d_ref, gw_ref, gb_ref,
                     cw_ref, gs_ref, gc_ref, e_ref, ys_ref, yc_ref,
                     bre_ref, bim_ref, st_ref, zb_ref, *, nb, unroll):
    rows = u_ref.shape[0]
    p = a_ref.shape[1]
    steps = rows // nb
    halo = (CONV_WIDTH - 1) * nb

    @pl.when(pl.program_id(0) == 0)
    def _():
        st_ref[...] = jnp.zeros_like(st_ref)
        zb_ref[0:halo, :] = jnp.zeros((halo, zb_ref.shape[1]), F32)

    u = u_ref[...]
    ub = u.astype(BF16)
    bre_ref[...] = _dot(ub, wb_ref[:, 0:p])
    bim_ref[...] = _dot(ub, wb_ref[:, p:2 * p])
    ar = jnp.broadcast_to(a_ref[0:1, :], (nb, p))
    ai = jnp.broadcast_to(a_ref[1:2, :], (nb, p))

    def step(t, carry):
        xr, xi = carry
        r = pl.multiple_of(t * nb, nb)
        nxr = ar * xr - ai * xi + bre_ref[pl.ds(r, nb), :]
        nxi = ar * xi + ai * xr + bim_ref[pl.ds(r, nb), :]
        bre_ref[pl.ds(r, nb), :] = nxr
        bim_ref[pl.ds(r, nb), :] = nxi
        return nxr, nxi

    xr, xi = lax.fori_loop(0, steps, step, (st_ref[0], st_ref[1]), unroll=unroll)
    st_ref[0] = xr
    st_ref[1] = xi

    y = _dot(bre_ref[...].astype(BF16), wc_ref[0:p, :]) + _dot(bim_ref[...].astype(BF16), wc_ref[p:2 * p, :])
    y = jax.nn.gelu(y + d_ref[...] * u)
    y = y * jax.nn.sigmoid(_dot(y.astype(BF16), gw_ref[...]) + gb_ref[...])
    e_mat = e_ref[...]
    ys_ref[...] = _head_norm(y, e_mat, gs_ref[...]).astype(ys_ref.dtype)

    zb_ref[halo:halo + rows, :] = cg_ref[...] * hc_ref[...]
    conv = cw_ref[0:1, :] * zb_ref[0:rows, :]
    for w in range(1, CONV_WIDTH):
        conv = conv + cw_ref[w:w + 1, :] * zb_ref[w * nb:w * nb + rows, :]
    tail = zb_ref[rows:rows + halo, :]
    zb_ref[0:halo, :] = tail
    yc_ref[...] = _head_norm(bg_ref[...] * conv, e_mat, gc_ref[...]).astype(yc_ref.dtype)


def _ssm_conv(u, hc, bg, cg, wb, wc, a_rows, d_skip, glu_w, glu_b, conv_w, g_ssm, g_conv, nb, rows):
    t, ds = u.shape
    dc = hc.shape[1]
    p = a_rows.shape[1]
    row = lambda i: (i, 0)
    fix = lambda i: (0, 0)
    full = lambda a: pl.BlockSpec(a.shape, fix)
    e_mat = _group_mean_matrix(ds)
    args = (u, hc, bg, cg, wb, wc, a_rows, d_skip.reshape(1, ds), glu_w.astype(BF16),
            glu_b.reshape(1, ds), conv_w, g_ssm.reshape(1, ds), g_conv.reshape(1, dc), e_mat)
    in_specs = [pl.BlockSpec((rows, ds), row)] + [pl.BlockSpec((rows, dc), row)] * 3
    in_specs += [full(a) for a in args[4:]]
    return pl.pallas_call(
        functools.partial(_ssm_conv_kernel, nb=nb, unroll=8),
        grid=(t // rows,),
        in_specs=in_specs,
        out_specs=[pl.BlockSpec((rows, ds), row), pl.BlockSpec((rows, dc), row)],
        out_shape=[jax.ShapeDtypeStruct((t, ds), BF16), jax.ShapeDtypeStruct((t, dc), BF16)],
        scratch_shapes=[pltpu.VMEM((rows, p), F32), pltpu.VMEM((rows, p), F32),
                        pltpu.VMEM((2, nb, p), F32),
                        pltpu.VMEM((rows + (CONV_WIDTH - 1) * nb, dc), F32)],
        compiler_params=_cparams(("arbitrary",), 56),
        name="ssm_conv",
    )(*args)


def _attn_kernel(q_ref, k_ref, v_ref, cum_ref, cumt_ref, qg_ref, kg_ref, og_ref, e_ref,
                 o_ref, kn_ref, *, tq, heads_per_batch):
    b = pl.program_id(0)
    hp = pl.program_id(1)
    qi = pl.program_id(2)
    e_mat = e_ref[...]
    lanes = q_ref.shape[1]
    half = lanes // 2

    @pl.when(qi == 0)
    def _():
        kf = k_ref[...].astype(F32)
        kn_ref[...] = _head_norm(kf, e_mat, kg_ref[...]).astype(BF16)

    qf = q_ref[...].astype(F32)
    qn = _head_norm(qf, e_mat, qg_ref[...]) * (HEAD_DIM ** -0.5)
    lane = lax.broadcasted_iota(jnp.int32, (tq, lanes), 1)
    first = lane < half
    q2 = jnp.concatenate([jnp.where(first, qn, 0.0), jnp.where(first, 0.0, qn)], axis=0).astype(BF16)

    col0 = b * heads_per_batch + 2 * hp
    n_cum = cum_ref.shape[1]
    srow = lax.broadcasted_iota(jnp.int32, (n_cum, lanes), 0)
    slane = lax.broadcasted_iota(jnp.int32, (n_cum, lanes), 1)
    sel = jnp.where(srow == col0 + jnp.where(slane >= half, 1, 0), 1.0, 0.0).astype(BF16)
    cq_full = _dot_exact_rhs(cum_ref[...], sel)
    cq = (cq_full[:, 0:1], cq_full[:, half:half + 1])

    rr = lax.broadcasted_iota(jnp.int32, (tq, tq), 0)
    cc = lax.broadcasted_iota(jnp.int32, (tq, tq), 1)
    causal = cc <= rr

    def block(ki, carry, masked):
        m, l, acc = carry
        off = pl.multiple_of(ki * tq, tq)
        kb = kn_ref[pl.ds(off, tq), :]
        vb = v_ref[pl.ds(off, tq), :]
        s = lax.dot_general(q2, kb, (((1,), (1,)), ((), ())), preferred_element_type=F32)
        parts = []
        for h in range(2):
            ck = cumt_ref[pl.ds(2 * hp + h, 1), pl.ds(off, tq)]
            sh = s[h * tq:(h + 1) * tq] + cq[h] - ck
            if masked:
                sh = jnp.where(causal, sh, -1e30)
            parts.append(sh)
        s = jnp.concatenate(parts, axis=0)
        m_new = jnp.maximum(m, jnp.max(s, axis=-1, keepdims=True))
        alpha = jnp.exp(m - m_new)
        pr = jnp.exp(s - m_new)
        l = alpha * l + jnp.sum(pr, axis=-1, keepdims=True)
        acc = alpha * acc + _dot(pr.astype(BF16), vb)
        return m_new, l, acc

    init = (jnp.full((2 * tq, 1), -1e30, F32), jnp.zeros((2 * tq, 1), F32),
            jnp.zeros((2 * tq, lanes), F32))
    carry = lax.fori_loop(0, qi, lambda ki, cr: block(ki, cr, False), init)
    _, l, acc = block(qi, carry, True)
    out2 = acc / l
    out = jnp.where(first, out2[:tq], out2[tq:])
    o_ref[...] = _head_norm(out, e_mat, og_ref[...]).astype(o_ref.dtype)


def _attention(q2d, k2d, v2d, cum, cum_t, q_g, k_g, o_g, n_batch, tq):
    s, width = q2d.shape
    per_batch = width // n_batch
    lanes = 2 * HEAD_DIM
    pairs = per_batch // lanes
    heads_per_batch = per_batch // HEAD_DIM
    e_mat = _group_mean_matrix(lanes)
    fix = lambda b, h, i: (0, 0)
    qmap = lambda b, h, i: (i, b * pairs + h)
    kmap = lambda b, h, i: (0, b * pairs + h)
    return pl.pallas_call(
        functools.partial(_attn_kernel, tq=tq, heads_per_batch=heads_per_batch),
        grid=(n_batch, pairs, s // tq),
        in_specs=[pl.BlockSpec((tq, lanes), qmap),
                  pl.BlockSpec((s, lanes), kmap),
                  pl.BlockSpec((s, lanes), kmap),
                  pl.BlockSpec((tq, cum.shape[1]), lambda b, h, i: (i, 0)),
                  pl.BlockSpec((heads_per_batch, s), lambda b, h, i: (b, 0)),
                  pl.BlockSpec((1, lanes), fix), pl.BlockSpec((1, lanes), fix),
                  pl.BlockSpec((1, lanes), lambda b, h, i: (0, h)),
                  pl.BlockSpec((lanes, lanes), fix)],
        out_specs=pl.BlockSpec((tq, lanes), qmap),
        out_shape=jax.ShapeDtypeStruct((s, width), BF16),
        scratch_shapes=[pltpu.VMEM((s, lanes), BF16)],
        compiler_params=_cparams(("arbitrary", "arbitrary", "arbitrary"), 48),
        name="fox_attention",
    )(q2d, k2d, v2d, cum, cum_t, jnp.tile(q_g, 2).reshape(1, lanes),
      jnp.tile(k_g, 2).reshape(1, lanes), o_g.reshape(1, per_batch), e_mat)


def _first_argmax(vals):
    best_i = jnp.zeros_like(vals[0], dtype=jnp.int32)
    best_v = vals[0]
    for i in range(1, len(vals)):
        better = vals[i] > best_v
        best_i = jnp.where(better, i, best_i)
        best_v = jnp.where(better, vals[i], best_v)
    return best_i, best_v


def _outproj_router_kernel(x_ref, ys_ref, ya_ref, yc_ref, wo_ref, g1_ref, sh_ref, sc_ref, n2_ref,
                           wr_ref, rb_ref, tri_ref, x1_ref, h2_ref, route_ref, cnt_ref):
    tm, d = x_ref.shape
    ds = ys_ref.shape[1]
    da = ya_ref.shape[1]
    y = _dot(ys_ref[...], wo_ref[0:ds, :]) + _dot(ya_ref[...], wo_ref[ds:ds + da, :])
    y = y + _dot(yc_ref[...], wo_ref[ds + da:, :])
    y3, g1 = _per_batch(y, g1_ref[...])
    x1 = x_ref[...] + (y3 * g1).reshape(tm, d)
    x1_ref[...] = x1
    h2 = _modulated_norm(x1, n2_ref[...], sh_ref[...], sc_ref[...])
    h2_ref[...] = h2.astype(BF16)

    logits = jnp.transpose(_dot_3x(h2, wr_ref[...]))
    n_exp = rb_ref.shape[0]
    aff = jax.nn.sigmoid(logits[0:n_exp, :])
    sel = aff + rb_ref[...]
    s_rows = [sel[e:e + 1, :] for e in range(n_exp)]
    a_rows = [aff[e:e + 1, :] for e in range(n_exp)]
    epg = EXPERTS_PER_GROUP
    scores = []
    for g in range(N_GROUPS):
        v = s_rows[g * epg:(g + 1) * epg]
        top2 = v[0] + v[1]
        for i in range(epg):
            for j in range(i + 1, epg):
                if (i, j) != (0, 1):
                    top2 = jnp.maximum(top2, v[i] + v[j])
        scores.append(top2)
    best, _ = _first_argmax(scores)

    def pick(rows_, i):
        out = rows_[i]
        for g in range(1, N_GROUPS):
            out = jnp.where(best == g, rows_[g * epg + i], out)
        return out

    sv = [pick(s_rows, i) for i in range(epg)]
    av = [pick(a_rows, i) for i in range(epg)]
    i1, _ = _first_argmax(sv)
    rest = [jnp.where(i1 == i, -jnp.inf, sv[i]) for i in range(epg)]
    i2, _ = _first_argmax(rest)
    chosen = [jnp.where((i1 == i) | (i2 == i), av[i], 0.0) for i in range(epg)]
    denom = chosen[0] + chosen[1] + chosen[2] + chosen[3]
    wts = [ch / denom for ch in chosen]

    grow = lax.broadcasted_iota(jnp.int32, (V7X_SUBLANES, tm), 0)
    member = grow == best
    cum8 = _dot(jnp.where(member, 1.0, 0.0).astype(BF16), tri_ref[...])
    rank = jnp.sum(jnp.where(member, cum8, 0.0), axis=0, keepdims=True)
    counts = jnp.sum(jnp.where(member, 1.0, 0.0), axis=1, keepdims=True)
    zero = jnp.zeros((1, tm), F32)
    for i, r in enumerate([best.astype(F32)] + wts + [rank, zero, zero]):
        route_ref[i:i + 1, :] = r
    cnt_ref[0] = jnp.broadcast_to(counts, (V7X_SUBLANES, V7X_LANES))


def _outproj_router(xs, ys, ya, yc, w_out, g1, shift, scale, norm_g, wr_pad, rbias, tri, tm):
    t, d = xs.shape
    b = g1.shape[0]
    row = lambda i: (i, 0)
    fix = lambda i: (0, 0)
    n_exp = rbias.shape[0]
    return pl.pallas_call(
        _outproj_router_kernel,
        grid=(t // tm,),
        in_specs=[pl.BlockSpec((tm, d), row), pl.BlockSpec((tm, ys.shape[1]), row),
                  pl.BlockSpec((tm, ya.shape[1]), row), pl.BlockSpec((tm, yc.shape[1]), row),
                  pl.BlockSpec(w_out.shape, fix), pl.BlockSpec((b, d), fix),
                  pl.BlockSpec((b, d), fix), pl.BlockSpec((b, d), fix), pl.BlockSpec((1, d), fix),
                  pl.BlockSpec(wr_pad.shape, fix), pl.BlockSpec((n_exp, 1), fix),
                  pl.BlockSpec((tm, tm), fix)],
        out_specs=[pl.BlockSpec((tm, d), row), pl.BlockSpec((tm, d), row),
                   pl.BlockSpec((V7X_SUBLANES, tm), lambda i: (0, i)),
                   pl.BlockSpec((1, V7X_SUBLANES, V7X_LANES), lambda i: (i, 0, 0))],
        out_shape=[jax.ShapeDtypeStruct((t, d), F32), jax.ShapeDtypeStruct((t, d), BF16),
                   jax.ShapeDtypeStruct((V7X_SUBLANES, t), F32),
                   jax.ShapeDtypeStruct((t // tm, V7X_SUBLANES, V7X_LANES), F32)],
        compiler_params=_cparams(("arbitrary",), 56),
        name="outproj_router",
    )(xs, ys, ya, yc, w_out, g1, shift, scale, norm_g.reshape(1, d), wr_pad,
      rbias.reshape(n_exp, 1), tri)


def _moe_kernel(cnt_ref, h_ref, route_ref, x1_ref, g2_ref, wg_ref, wu_ref, wd_ref, o_ref, *, chunk):
    j = pl.program_id(0)
    g = pl.program_id(1)
    tm, d = h_ref.shape

    @pl.when(g == 0)
    def _():
        o_ref[...] = jnp.zeros_like(o_ref)

    n = cnt_ref[j * N_GROUPS + g]
    n_chunks = (n + chunk - 1) // chunk
    member = route_ref[0:1, :] == g.astype(F32)
    rank = jnp.where(member, route_ref[EXPERTS_PER_GROUP + 1:EXPERTS_PER_GROUP + 2, :], -1.0)
    slot = lax.broadcasted_iota(jnp.int32, (chunk, 1), 0).astype(F32)

    def body(c, carry):
        onehot = rank == slot + (c * chunk).astype(F32)
        perm = jnp.where(onehot, 1.0, 0.0).astype(BF16)
        hc = _dot(perm, h_ref[...]).astype(BF16)
        y = jnp.zeros((chunk, d), F32)
        for e in range(EXPERTS_PER_GROUP):
            w_e = jnp.sum(jnp.where(onehot, route_ref[1 + e:2 + e, :], 0.0), axis=1, keepdims=True)
            hid = jax.nn.silu(_dot(hc, wg_ref[e])) * _dot(hc, wu_ref[e])
            y = y + _dot((hid * w_e).astype(BF16), wd_ref[e])
        o_ref[...] += lax.dot_general(perm, y.astype(BF16), (((0,), (0,)), ((), ())),
                                      preferred_element_type=F32)
        return carry

    lax.fori_loop(0, n_chunks, body, 0)

    @pl.when(g == pl.num_programs(1) - 1)
    def _():
        o3, g2 = _per_batch(o_ref[...], g2_ref[...])
        o_ref[...] = x1_ref[...] + (o3 * g2).reshape(tm, d)


def _moe(counts, h2, route, x1, g2, w_gate, w_up, w_down, tm, chunk):
    t, d = h2.shape
    b = g2.shape[0]
    f = w_gate.shape[2]
    epg = EXPERTS_PER_GROUP
    grid_spec = pltpu.PrefetchScalarGridSpec(
        num_scalar_prefetch=1,
        grid=(t // tm, N_GROUPS),
        in_specs=[pl.BlockSpec((tm, d), lambda j, g, c: (j, 0)),
                  pl.BlockSpec((V7X_SUBLANES, tm), lambda j, g, c: (0, j)),
                  pl.BlockSpec((tm, d), lambda j, g, c: (j, 0)),
                  pl.BlockSpec((b, d), lambda j, g, c: (0, 0)),
                  pl.BlockSpec((epg, d, f), lambda j, g, c: (g, 0, 0)),
                  pl.BlockSpec((epg, d, f), lambda j, g, c: (g, 0, 0)),
                  pl.BlockSpec((epg, f, d), lambda j, g, c: (g, 0, 0))],
        out_specs=pl.BlockSpec((tm, d), lambda j, g, c: (j, 0)),
    )
    return pl.pallas_call(
        functools.partial(_moe_kernel, chunk=chunk),
        grid_spec=grid_spec,
        out_shape=jax.ShapeDtypeStruct((t, d), F32),
        compiler_params=_cparams(("arbitrary", "arbitrary"), 60),
        name="moe",
    )(counts, h2, route, x1, g2, w_gate, w_up, w_down)


def _ssm_params(lam_re, lam_im, log_dt, b_re, b_im, c_re, c_im):
    g, p = lam_re.shape
    h = b_re.shape[2]
    lr = jnp.minimum(lam_re, -1e-4)
    li = lam_im
    dt = jnp.exp(log_dt)[:, None]
    mag = jnp.exp(lr * dt)
    ab_re = mag * jnp.cos(li * dt)
    ab_im = mag * jnp.sin(li * dt)
    den = lr * lr + li * li
    num_re = ab_re - 1.0
    num_im = ab_im
    z_re = (num_re * lr + num_im * li) / den
    z_im = (num_im * lr - num_re * li) / den
    bb_re = z_re[..., None] * b_re - z_im[..., None] * b_im
    bb_im = z_re[..., None] * b_im + z_im[..., None] * b_re
    eye = jnp.eye(g, dtype=F32)

    def in_blockdiag(m):
        return (m.transpose(0, 2, 1)[:, :, None, :] * eye[:, None, :, None]).reshape(g * h, g * p)

    def out_blockdiag(m):
        return (m.transpose(0, 2, 1)[:, :, None, :] * eye[:, None, :, None]).reshape(g * p, g * h)

    wb = jnp.concatenate([in_blockdiag(bb_re), in_blockdiag(bb_im)], axis=1).astype(BF16)
    wc = jnp.concatenate([out_blockdiag(c_re), -out_blockdiag(c_im)], axis=0).astype(BF16)
    a_rows = jnp.stack([ab_re.reshape(g * p), ab_im.reshape(g * p)], axis=0)
    return wb, wc, a_rows


def _pick_tile(n, target):
    t = min(n, target)
    while n % t:
        t //= 2
    return t


def kernel(x, c, ada_w, ada_b, norm1_g, w_in, forget_b, lam_re, lam_im, log_dt, ssm_b_re, ssm_b_im,
           ssm_c_re, ssm_c_im, ssm_d, glu_w, glu_b, q_norm_g, k_norm_g, conv_w, out_norm_g, w_out,
           norm2_g, w_router, router_bias, w_gate, w_up, w_down):
    n_batch, seq, d = x.shape
    depth = ada_w.shape[0]
    assert n_batch == V7X_SUBLANES, "token rows are tiled one timestep per sublane tile"
    t = n_batch * seq
    d_ssm = ssm_d.shape[1]
    d_conv = conv_w.shape[2]
    n_heads = forget_b.shape[1]
    d_attn = n_heads * HEAD_DIM
    n_exp = w_router.shape[1]
    assert n_exp == N_GROUPS * EXPERTS_PER_GROUP
    widths = (d_ssm, d_attn, d_attn, d_attn, d_conv, d_conv, d_conv)
    core = d_ssm + 3 * d_attn

    tm_proj = _pick_tile(t, 1024)
    tm_moe = _pick_tile(t, 1024)
    rows_ssm = _pick_tile(t, 2048)
    tq = _pick_tile(seq, 512)
    moe_chunk = 304 if tm_moe == 1024 else _pick_tile(tm_moe, 256)

    mod = _adaln(c, ada_w, ada_b)
    xs = x.transpose(1, 0, 2).reshape(t, d)
    tri = jnp.where(jnp.arange(tm_moe)[:, None] < jnp.arange(tm_moe)[None, :], 1.0, 0.0).astype(BF16)
    wr_pad = jnp.pad(w_router, ((0, 0), (0, V7X_LANES - n_exp)))
    fb_row = jnp.tile(forget_b, (1, n_batch))

    for l in range(depth):
        sh1, sc1, g1, sh2, sc2, g2 = [mod[l, :, i * d:(i + 1) * d] for i in range(6)]
        wl = w_in[l]
        w_perm = jnp.concatenate(
            [wl[:, :core], wl[:, core + n_heads:], wl[:, core:core + n_heads],
             jnp.zeros((d, V7X_LANES - n_heads), F32)], axis=1).astype(BF16)
        u, q, k, v, hc, bg, cg, f = _inproj(xs, norm1_g[l], sh1, sc1, w_perm, widths, n_heads, tm_proj)

        cum = _forget_cumsum(f.reshape(seq, n_batch * n_heads), fb_row[l:l + 1])
        wb, wc, a_rows = _ssm_params(lam_re[l], lam_im[l], log_dt[l], ssm_b_re[l], ssm_b_im[l],
                                     ssm_c_re[l], ssm_c_im[l])
        og = out_norm_g[l]
        ys, yc = _ssm_conv(u, hc, bg, cg, wb, wc, a_rows, ssm_d[l], glu_w[l], glu_b[l], conv_w[l],
                           og[:d_ssm], og[d_ssm + d_attn:], n_batch, rows_ssm)
        wide = lambda a: a.reshape(seq, n_batch * d_attn)
        ya = _attention(wide(q), wide(k), wide(v), cum, cum.T, q_norm_g[l], k_norm_g[l],
                        og[d_ssm:d_ssm + d_attn], n_batch, tq)
        x1, h2, route, cnt = _outproj_router(xs, ys, ya.reshape(t, d_attn), yc, w_out[l].astype(BF16),
                                             g1, sh2, sc2, norm2_g[l], wr_pad, router_bias, tri, tm_moe)
        counts = cnt[:, :N_GROUPS, 0].astype(jnp.int32).reshape(-1)
        xs = _moe(counts, h2, route, x1, g2, w_gate[l].astype(BF16), w_up[l].astype(BF16),
                  w_down[l].astype(BF16), tm_moe, moe_chunk)

    return xs.reshape(seq, n_batch, d).transpose(1, 0, 2)
```
